```python
import jax, jax.numpy as jnp
from jax import lax
import numpy as np

D_MODEL = 1024
BATCH = 16
SEQ = 4096
DEPTH = 2
DEC_BATCH = 8
DEC_SEQ = 8192
PAST_LEN = 128

N_EVEN = (DEPTH + 1) // 2
N_ODD = DEPTH // 2
D_FF = 2816
D_MIX = D_MODEL
D_A = D_MIX // 2
D_B = D_MIX - D_A
A_HEADS = 8
A_HEAD_DIM = D_A // A_HEADS
CHUNK = 128
B_CONV_W = 3
D_CONV = D_MODEL
CONV_W = 31
D_IN_AB = 2 * D_A + 3 * D_B
RMS_EPS = 1e-6
LN_EPS = 1e-5

kernel_name = "hybrid_gmlp_shortconv_conformer_encoder"


def rmsnorm(x, g):
    xf = x.astype(jnp.float32)
    y = xf * lax.rsqrt(jnp.mean(xf * xf, axis=-1, keepdims=True) + RMS_EPS)
    return (y * g.astype(jnp.float32)).astype(x.dtype)


def layernorm(x, g, b):
    xf = x.astype(jnp.float32)
    mu = jnp.mean(xf, axis=-1, keepdims=True)
    var = jnp.mean(jnp.square(xf - mu), axis=-1, keepdims=True)
    y = (xf - mu) * lax.rsqrt(var + LN_EPS)
    return (y * g.astype(jnp.float32) + b.astype(jnp.float32)).astype(x.dtype)


def swiglu(x, w_gate, w_up, w_down):
    return (jax.nn.silu(x @ w_gate) * (x @ w_up)) @ w_down


def depthwise_conv(x, w):
    c = x.shape[-1]
    return lax.conv_general_dilated(
        x, w[:, None, :].astype(x.dtype), window_strides=(1,), padding="SAME",
        dimension_numbers=("NWC", "WIO", "NWC"), feature_group_count=c)


def chunked_spatial_gate(u, v, ws, bs):
    b, s, _ = u.shape
    shp = (b, s // CHUNK, CHUNK, A_HEADS, A_HEAD_DIM)
    vc = v.reshape(shp)
    mixed = jnp.einsum("hpq,bcqhd->bcphd", ws, vc) + bs.T[None, None, :, :, None]
    return (u.reshape(shp) * mixed).reshape(b, s, D_A)


def mixer_ab(h, w_in, a_ws, a_bs, b_conv, w_out):
    z = h @ w_in
    za = jax.nn.gelu(z[..., :2 * D_A])
    u, v = za[..., :D_A], za[..., D_A:]
    b_gate = z[..., 2 * D_A:2 * D_A + D_B]
    c_gate = z[..., 2 * D_A + D_B:2 * D_A + 2 * D_B]
    xin = z[..., 2 * D_A + 2 * D_B:]
    y_a = chunked_spatial_gate(u, v, a_ws, a_bs)
    y_b = b_gate * depthwise_conv(c_gate * xin, b_conv)
    return jnp.concatenate([y_a, y_b], axis=-1) @ w_out


def conformer_conv(h, w_pw1, b_pw1, dw_w, dw_b, ln_g, ln_b, w_pw2, b_pw2):
    z = h @ w_pw1 + b_pw1
    z = z[..., :D_CONV] * jax.nn.sigmoid(z[..., D_CONV:])
    z = depthwise_conv(z, dw_w) + dw_b
    z = jax.nn.silu(layernorm(z, ln_g, ln_b))
    return z @ w_pw2 + b_pw2


def trunk(x, ffn_norm, ffn_w_gate, ffn_w_up, ffn_w_down, mix_norm,
          ab_w_in, a_spatial_w, a_spatial_b, b_conv_w, ab_w_out,
          c_w_pw1, c_b_pw1, c_dw_w, c_dw_b, c_ln_g, c_ln_b, c_w_pw2, c_b_pw2, final_norm):
    for l in range(DEPTH):
        x = x + 0.5 * swiglu(rmsnorm(x, ffn_norm[l, 0]), ffn_w_gate[l, 0], ffn_w_up[l, 0], ffn_w_down[l, 0])
        h = rmsnorm(x, mix_norm[l])
        i = l // 2
        if l % 2 == 0:
            x = x + mixer_ab(h, ab_w_in[i], a_spatial_w[i], a_spatial_b[i], b_conv_w[i], ab_w_out[i])
        else:
            x = x + conformer_conv(h, c_w_pw1[i], c_b_pw1[i], c_dw_w[i], c_dw_b[i],
                                   c_ln_g[i], c_ln_b[i], c_w_pw2[i], c_b_pw2[i])
        x = x + 0.5 * swiglu(rmsnorm(x, ffn_norm[l, 1]), ffn_w_gate[l, 1], ffn_w_up[l, 1], ffn_w_down[l, 1])
    return rmsnorm(x, final_norm)


def setup_inputs(seed: int = 0) -> dict:
    key = jax.random.key(seed)
    ks = jax.random.split(key, 24)
    f32 = jnp.float32

    def nrm(k, shape, scale):
        return jax.random.normal(k, shape, f32) * scale

    return {
        "x_prompt": nrm(ks[0], (BATCH, SEQ, D_MODEL), 1.0),
        "x_sample": nrm(ks[1], (DEC_BATCH, DEC_SEQ, D_MODEL), 1.0),
        "ffn_norm": 1.0 + nrm(ks[2], (DEPTH, 2, D_MODEL), 0.02),
        "ffn_w_gate": nrm(ks[3], (DEPTH, 2, D_MODEL, D_FF), D_MODEL ** -0.5),
        "ffn_w_up": nrm(ks[4], (DEPTH, 2, D_MODEL, D_FF), D_MODEL ** -0.5),
        "ffn_w_down": nrm(ks[5], (DEPTH, 2, D_FF, D_MODEL), D_FF ** -0.5),
        "mix_norm": 1.0 + nrm(ks[6], (DEPTH, D_MODEL), 0.02),
        "ab_w_in": nrm(ks[7], (N_EVEN, D_MODEL, D_IN_AB), D_MODEL ** -0.5),
        "a_spatial_w": nrm(ks[8], (N_EVEN, A_HEADS, CHUNK, CHUNK), CHUNK ** -0.5),
        "a_spatial_b": 1.0 + nrm(ks[9], (N_EVEN, A_HEADS, CHUNK), 0.1),
        "b_conv_w": nrm(ks[10], (N_EVEN, B_CONV_W, D_B), B_CONV_W ** -0.5),
        "ab_w_out": nrm(ks[11], (N_EVEN, D_MIX, D_MODEL), D_MIX ** -0.5),
        "c_w_pw1": nrm(ks[12], (N_ODD, D_MODEL, 2 * D_CONV), D_MODEL ** -0.5),
        "c_b_pw1": nrm(ks[13], (N_ODD, 2 * D_CONV), 0.01),
        "c_dw_w": nrm(ks[14], (N_ODD, CONV_W, D_CONV), CONV_W ** -0.5),
        "c_dw_b": nrm(ks[15], (N_ODD, D_CONV), 0.01),
        "c_ln_g": 1.0 + nrm(ks[16], (N_ODD, D_CONV), 0.02),
        "c_ln_b": nrm(ks[17], (N_ODD, D_CONV), 0.01),
        "c_w_pw2": nrm(ks[18], (N_ODD, D_CONV, D_MODEL), D_CONV ** -0.5),
        "c_b_pw2": nrm(ks[19], (N_ODD, D_MODEL), 0.01),
        "final_norm": 1.0 + nrm(ks[20], (D_MODEL,), 0.02),
    }


def reference(x_prompt, x_sample, ffn_norm, ffn_w_gate, ffn_w_up, ffn_w_down, mix_norm,
              ab_w_in, a_spatial_w, a_spatial_b, b_conv_w, ab_w_out,
              c_w_pw1, c_b_pw1, c_dw_w, c_dw_b, c_ln_g, c_ln_b, c_w_pw2, c_b_pw2, final_norm):
    y_prompt = trunk(x_prompt, ffn_norm, ffn_w_gate, ffn_w_up, ffn_w_down, mix_norm,
                     ab_w_in, a_spatial_w, a_spatial_b, b_conv_w, ab_w_out,
                     c_w_pw1, c_b_pw1, c_dw_w, c_dw_b, c_ln_g, c_ln_b, c_w_pw2, c_b_pw2, final_norm)
    y_sample = trunk(x_sample, ffn_norm, ffn_w_gate, ffn_w_up, ffn_w_down, mix_norm,
                     ab_w_in, a_spatial_w, a_spatial_b, b_conv_w, ab_w_out,
                     c_w_pw1, c_b_pw1, c_dw_w, c_dw_b, c_ln_g, c_ln_b, c_w_pw2, c_b_pw2, final_norm)
    return (y_prompt, y_sample)
```

```python
import functools

import jax
import jax.numpy as jnp
from jax import lax
from jax.experimental import pallas as pl
from jax.experimental.pallas import tpu as pltpu

D_MODEL = 1024
D_FF = 2816
D_A = 512
D_B = 512
A_HEADS = 8
A_HEAD_DIM = 64
CHUNK = 128
B_CONV_W = 3
D_CONV = 1024
CONV_W = 31
RMS_EPS = 1e-6
LN_EPS = 1e-5

V7X_LANES = 128
V7X_F32_SUBLANES = 8
V7X_MXU_DIM = 256
V7X_VMEM_LIMIT_BYTES = 56 * 1024 * 1024

TM = 512
FF_CHUNK = V7X_MXU_DIM
AB_HALO = V7X_F32_SUBLANES
C_HALO = 16

_BF16 = jnp.bfloat16
_F32 = jnp.float32


def _rmsnorm(x, g):
    ms = jnp.mean(x * x, axis=-1, keepdims=True)
    return x * lax.rsqrt(ms + RMS_EPS) * g


def _dot(a, b):
    return jnp.dot(a, b, preferred_element_type=_F32)


def _ffn_kernel(x_ref, g_ref, wg_ref, wu_ref, wd_ref, fg_ref, o_ref, a_ref, *, final_norm):
    x = x_ref[...]
    h = _rmsnorm(x, g_ref[...]).astype(_BF16)
    for c in range(D_FF // FF_CHUNK):
        cols = slice(c * FF_CHUNK, (c + 1) * FF_CHUNK)
        gate = _dot(h, wg_ref[:, cols])
        up = _dot(h, wu_ref[:, cols])
        a_ref[:, cols] = (gate * jax.nn.sigmoid(gate) * up).astype(_BF16)
    y = x + 0.5 * _dot(a_ref[...], wd_ref[...])
    if final_norm:
        y = _rmsnorm(y, fg_ref[...])
    o_ref[...] = y


def _resident(shape):
    zeros = (0,) * len(shape)
    return pl.BlockSpec(shape, lambda i: zeros, pipeline_mode=pl.Buffered(1))


def _row_tile():
    return pl.BlockSpec((TM, D_MODEL), lambda i: (i, 0))


def _params():
    return pltpu.CompilerParams(dimension_semantics=("parallel",),
                                vmem_limit_bytes=V7X_VMEM_LIMIT_BYTES)


def _ffn(x, g, wg, wu, wd, fg, final_norm):
    n = x.shape[0]
    return pl.pallas_call(
        functools.partial(_ffn_kernel, final_norm=final_norm),
        grid=(n // TM,),
        in_specs=[_row_tile(), _resident((1, D_MODEL)), _resident((D_MODEL, D_FF)),
                  _resident((D_MODEL, D_FF)), _resident((D_FF, D_MODEL)), _resident((1, D_MODEL))],
        out_specs=_row_tile(),
        out_shape=jax.ShapeDtypeStruct((n, D_MODEL), _F32),
        scratch_shapes=[pltpu.VMEM((TM, D_FF), _BF16)],
        compiler_params=_params(),
        name="ffn_final" if final_norm else "ffn",
    )(x, g, wg, wu, wd, fg)


def _halo_specs(halo):
    per_tile = TM // halo

    def prev_map(i):
        return (jnp.maximum(i * per_tile - 1, 0), 0)

    def next_map(i):
        return (jnp.minimum((i + 1) * per_tile, pl.num_programs(0) * per_tile - 1), 0)

    return (pl.BlockSpec((halo, D_MODEL), prev_map), pl.BlockSpec((halo, D_MODEL), next_map))


def _mixer_ab_kernel(x_ref, xp_ref, xn_ref, g_ref, win_ref, ws_ref, bias_ref, cw_ref, wout_ref,
                     o_ref, y_ref, s_ref, *, tiles_per_seq):
    i = pl.program_id(0)
    seq_first = (i % tiles_per_seq) == 0
    seq_last = (i % tiles_per_seq) == tiles_per_seq - 1
    g = g_ref[...]
    x = x_ref[...]
    h = _rmsnorm(x, g).astype(_BF16)

    za = jax.nn.gelu(_dot(h, win_ref[:, :2 * D_A]))
    u = za[:, :D_A]
    v = za[:, D_A:].astype(_BF16)
    n_chunks = TM // CHUNK
    lane = lax.broadcasted_iota(jnp.int32, (CHUNK, n_chunks * V7X_LANES), 1)
    low_half = (lane % V7X_LANES) < A_HEAD_DIM
    for j in range(D_A // V7X_LANES):
        lanes = slice(j * V7X_LANES, (j + 1) * V7X_LANES)
        vcat = jnp.concatenate([v[c * CHUNK:(c + 1) * CHUNK, lanes] for c in range(n_chunks)], axis=1)
        mixed = jnp.where(low_half, _dot(ws_ref[2 * j], vcat), _dot(ws_ref[2 * j + 1], vcat))
        for c in range(n_chunks):
            rows = slice(c * CHUNK, (c + 1) * CHUNK)
            m = mixed[:, c * V7X_LANES:(c + 1) * V7X_LANES] + bias_ref[:, lanes]
            y_ref[rows, lanes] = (u[rows, lanes] * m).astype(_BF16)

    zb = _dot(h, win_ref[:, 2 * D_A:])
    b_gate = zb[:, :D_B]
    s_ref[AB_HALO:AB_HALO + TM, :] = zb[:, D_B:2 * D_B] * zb[:, 2 * D_B:]
    xh = jnp.concatenate([xp_ref[...], xn_ref[...]], axis=0)
    zh = _dot(_rmsnorm(xh, g).astype(_BF16), win_ref[:, 2 * D_A + D_B:])
    sh = zh[:, :D_B] * zh[:, D_B:]
    s_ref[0:AB_HALO, :] = jnp.where(seq_first, 0.0, sh[:AB_HALO])
    s_ref[AB_HALO + TM:, :] = jnp.where(seq_last, 0.0, sh[AB_HALO:])
    conv = (cw_ref[0:1, :] * s_ref[AB_HALO - 1:AB_HALO - 1 + TM, :]
            + cw_ref[1:2, :] * s_ref[AB_HALO:AB_HALO + TM, :]
            + cw_ref[2:3, :] * s_ref[AB_HALO + 1:AB_HALO + 1 + TM, :])
    y_ref[:, D_A:] = (b_gate * conv).astype(_BF16)

    o_ref[...] = x + _dot(y_ref[...], wout_ref[...])


def _mixer_ab(x, seq_len, g, win, ws, bias, cw, wout):
    n = x.shape[0]
    prev_spec, next_spec = _halo_specs(AB_HALO)
    return pl.pallas_call(
        functools.partial(_mixer_ab_kernel, tiles_per_seq=seq_len // TM),
        grid=(n // TM,),
        in_specs=[_row_tile(), prev_spec, next_spec, _resident((1, D_MODEL)),
                  _resident((D_MODEL, 2 * D_A + 3 * D_B)), _resident((A_HEADS, CHUNK, CHUNK)),
                  _resident((CHUNK, D_A)), _resident((B_CONV_W, D_B)), _resident((D_MODEL, D_MODEL))],
        out_specs=_row_tile(),
        out_shape=jax.ShapeDtypeStruct((n, D_MODEL), _F32),
        scratch_shapes=[pltpu.VMEM((TM, D_MODEL), _BF16),
                        pltpu.VMEM((TM + 2 * AB_HALO, D_B), _F32)],
        compiler_params=_params(),
        name="mixer_ab",
    )(x, x, x, g, win, ws, bias, cw, wout)


def _conformer_kernel(x_ref, xp_ref, xn_ref, g_ref, w1_ref, b1_ref, dw_ref, dwb_ref, lng_ref, lnb_ref,
                      w2_ref, b2_ref, o_ref, z_ref, *, tiles_per_seq):
    i = pl.program_id(0)
    seq_first = (i % tiles_per_seq) == 0
    seq_last = (i % tiles_per_seq) == tiles_per_seq - 1
    x = x_ref[...]
    xe = jnp.concatenate([xp_ref[...], x, xn_ref[...]], axis=0)
    h = _rmsnorm(xe, g_ref[...]).astype(_BF16)
    z = _dot(h, w1_ref[...]) + b1_ref[...]
    glu = z[:, :D_CONV] * jax.nn.sigmoid(z[:, D_CONV:])
    z_ref[...] = glu
    z_ref[0:C_HALO, :] = jnp.where(seq_first, 0.0, glu[:C_HALO])
    z_ref[C_HALO + TM:, :] = jnp.where(seq_last, 0.0, glu[C_HALO + TM:])
    half = CONV_W // 2
    acc = jnp.zeros((TM, D_CONV), _F32)
    for k in range(CONV_W):
        off = C_HALO - half + k
        acc = acc + dw_ref[k:k + 1, :] * z_ref[off:off + TM, :]
    acc = acc + dwb_ref[...]
    mu = jnp.mean(acc, axis=-1, keepdims=True)
    cen = acc - mu
    var = jnp.mean(cen * cen, axis=-1, keepdims=True)
    ln = cen * lax.rsqrt(var + LN_EPS) * lng_ref[...] + lnb_ref[...]
    act = (ln * jax.nn.sigmoid(ln)).astype(_BF16)
    o_ref[...] = x + _dot(act, w2_ref[...]) + b2_ref[...]


def _conformer(x, seq_len, g, w1, b1, dw, dwb, lng, lnb, w2, b2):
    n = x.shape[0]
    prev_spec, next_spec = _halo_specs(C_HALO)
    vec = _resident((1, D_CONV))
    return pl.pallas_call(
        functools.partial(_conformer_kernel, tiles_per_seq=seq_len // TM),
        grid=(n // TM,),
        in_specs=[_row_tile(), prev_spec, next_spec, _resident((1, D_MODEL)),
                  _resident((D_MODEL, 2 * D_CONV)), _resident((1, 2 * D_CONV)),
                  _resident((CONV_W, D_CONV)), vec, vec, vec,
                  _resident((D_CONV, D_MODEL)), _resident((1, D_MODEL))],
        out_specs=_row_tile(),
        out_shape=jax.ShapeDtypeStruct((n, D_MODEL), _F32),
        scratch_shapes=[pltpu.VMEM((TM + 2 * C_HALO, D_CONV), _F32)],
        compiler_params=_params(),
        name="conformer",
    )(x, x, x, g, w1, b1, dw, dwb, lng, lnb, w2, b2)


def _trunk(x, w):
    batch, seq_len, _ = x.shape
    assert seq_len % TM == 0 and TM % CHUNK == 0
    x = x.reshape(batch * seq_len, D_MODEL)
    depth = w["ffn_norm"].shape[0]
    for l in range(depth):
        x = _ffn(x, w["ffn_norm"][l, 0][None], w["ffn_w_gate"][l, 0], w["ffn_w_up"][l, 0],
                 w["ffn_w_down"][l, 0], w["final_norm"][None], final_norm=False)
        g = w["mix_norm"][l][None]
        i = l // 2
        if l % 2 == 0:
            x = _mixer_ab(x, seq_len, g, w["ab_w_in"][i], w["a_spatial_w"][i], w["a_bias"][i],
                          w["b_conv_w"][i], w["ab_w_out"][i])
        else:
            x = _conformer(x, seq_len, g, w["c_w_pw1"][i], w["c_b_pw1"][i][None], w["c_dw_w"][i],
                           w["c_dw_b"][i][None], w["c_ln_g"][i][None], w["c_ln_b"][i][None],
                           w["c_w_pw2"][i], w["c_b_pw2"][i][None])
        x = _ffn(x, w["ffn_norm"][l, 1][None], w["ffn_w_gate"][l, 1], w["ffn_w_up"][l, 1],
                 w["ffn_w_down"][l, 1], w["final_norm"][None], final_norm=(l == depth - 1))
    return x.reshape(batch, seq_len, D_MODEL)


def kernel(x_prompt, x_sample, ffn_norm, ffn_w_gate, ffn_w_up, ffn_w_down, mix_norm, ab_w_in, a_spatial_w, a_spatial_b, b_conv_w, ab_w_out, c_w_pw1, c_b_pw1, c_dw_w, c_dw_b, c_ln_g, c_ln_b, c_w_pw2, c_b_pw2, final_norm):
    w = dict(
        ffn_norm=ffn_norm, mix_norm=mix_norm, final_norm=final_norm,
        ffn_w_gate=ffn_w_gate.astype(_BF16), ffn_w_up=ffn_w_up.astype(_BF16),
        ffn_w_down=ffn_w_down.astype(_BF16),
        ab_w_in=ab_w_in.astype(_BF16), a_spatial_w=a_spatial_w.astype(_BF16),
        a_bias=jnp.repeat(jnp.swapaxes(a_spatial_b, 1, 2), A_HEAD_DIM, axis=2),
        b_conv_w=b_conv_w, ab_w_out=ab_w_out.astype(_BF16),
        c_w_pw1=c_w_pw1.astype(_BF16), c_b_pw1=c_b_pw1, c_dw_w=c_dw_w, c_dw_b=c_dw_b,
        c_ln_g=c_ln_g, c_ln_b=c_ln_b, c_w_pw2=c_w_pw2.astype(_BF16), c_b_pw2=c_b_pw2,
    )
    return (_trunk(x_prompt, w), _trunk(x_sample, w))
```

```python
import functools

import jax
import jax.numpy as jnp
from jax import lax
from jax.experimental import pallas as pl
from jax.experimental.pallas import tpu as pltpu

D_MODEL = 1024
D_FF = 2816
D_A = 512
D_B = 512
A_HEADS = 8
A_HEAD_DIM = 64
CHUNK = 128
B_CONV_W = 3
D_CONV = 1024
CONV_W = 31
RMS_EPS = 1e-6
LN_EPS = 1e-5

V7X_LANES = 128
V7X_F32_SUBLANES = 8
V7X_BF16_SUBLANES = 16
V7X_MXU_DIM = 256
V7X_VMEM_LIMIT_BYTES = 56 * 1024 * 1024

TM_FFN = 1024
TM_AB = 512
TM_C = 512
FF_CHUNK = V7X_MXU_DIM
AB_HALO = V7X_F32_SUBLANES
C_HALO = 16
C_SLABS = D_CONV // V7X_LANES
C_ACCS = 4
C_GROUPS_PER_ITER = 16

_BF16 = jnp.bfloat16
_F32 = jnp.float32


def _rmsnorm(x, g):
    ms = jnp.mean(x * x, axis=-1, keepdims=True)
    return x * lax.rsqrt(ms + RMS_EPS) * g


def _dot(a, b):
    return jnp.dot(a, b, preferred_element_type=_F32)


def _resident(shape):
    zeros = (0,) * len(shape)
    return pl.BlockSpec(shape, lambda i: zeros, pipeline_mode=pl.Buffered(1))


def _row_tile(tm):
    return pl.BlockSpec((tm, D_MODEL), lambda i: (i, 0))


def _halo_specs(tm, halo):
    per_tile = tm // halo

    def prev_map(i):
        return (jnp.maximum(i * per_tile - 1, 0), 0)

    def next_map(i):
        return (jnp.minimum((i + 1) * per_tile, pl.num_programs(0) * per_tile - 1), 0)

    return (pl.BlockSpec((halo, D_MODEL), prev_map), pl.BlockSpec((halo, D_MODEL), next_map))


def _params():
    return pltpu.CompilerParams(dimension_semantics=("parallel",),
                                vmem_limit_bytes=V7X_VMEM_LIMIT_BYTES)


def _ffn_kernel(x_ref, g_ref, wg_ref, wu_ref, wd_ref, fg_ref, o_ref, a_ref, *, final_norm):
    x = x_ref[...]
    h = _rmsnorm(x, g_ref[...]).astype(_BF16)
    for c in range(D_FF // FF_CHUNK):
        cols = slice(c * FF_CHUNK, (c + 1) * FF_CHUNK)
        gate = _dot(h, wg_ref[:, cols])
        up = _dot(h, wu_ref[:, cols])
        a_ref[:, cols] = (gate * jax.nn.sigmoid(gate) * up).astype(_BF16)
    y = x + 0.5 * _dot(a_ref[...], wd_ref[...])
    if final_norm:
        y = _rmsnorm(y, fg_ref[...])
    o_ref[...] = y


def _ffn(x, g, wg, wu, wd, fg, final_norm):
    n = x.shape[0]
    return pl.pallas_call(
        functools.partial(_ffn_kernel, final_norm=final_norm),
        grid=(n // TM_FFN,),
        in_specs=[_row_tile(TM_FFN), _resident((1, D_MODEL)), _resident((D_MODEL, D_FF)),
                  _resident((D_MODEL, D_FF)), _resident((D_FF, D_MODEL)), _resident((1, D_MODEL))],
        out_specs=_row_tile(TM_FFN),
        out_shape=jax.ShapeDtypeStruct((n, D_MODEL), _F32),
        scratch_shapes=[pltpu.VMEM((TM_FFN, D_FF), _BF16)],
        compiler_params=_params(),
        name="ffn_final" if final_norm else "ffn",
    )(x, g, wg, wu, wd, fg)


def _mixer_ab_kernel(x_ref, xp_ref, xn_ref, g_ref, win_ref, ws_ref, bias_ref, cw_ref, wout_ref,
                     o_ref, y_ref, s_ref, *, tiles_per_seq):
    tm = TM_AB
    i = pl.program_id(0)
    seq_first = (i % tiles_per_seq) == 0
    seq_last = (i % tiles_per_seq) == tiles_per_seq - 1
    g = g_ref[...]
    x = x_ref[...]
    h = _rmsnorm(x, g).astype(_BF16)

    za = _dot(h, win_ref[:, :2 * D_A])
    zb = _dot(h, win_ref[:, 2 * D_A:])
    xh = jnp.concatenate([xp_ref[...], xn_ref[...]], axis=0)
    zh = _dot(_rmsnorm(xh, g).astype(_BF16), win_ref[:, 2 * D_A + D_B:])

    za = jax.nn.gelu(za)
    u = za[:, :D_A]
    v = za[:, D_A:].astype(_BF16)
    n_chunks = tm // CHUNK
    lane = lax.broadcasted_iota(jnp.int32, (CHUNK, n_chunks * V7X_LANES), 1)
    low_half = (lane % V7X_LANES) < A_HEAD_DIM
    for j in range(D_A // V7X_LANES):
        lanes = slice(j * V7X_LANES, (j + 1) * V7X_LANES)
        vcat = jnp.concatenate([v[c * CHUNK:(c + 1) * CHUNK, lanes] for c in range(n_chunks)], axis=1)
        mixed = jnp.where(low_half, _dot(ws_ref[2 * j], vcat), _dot(ws_ref[2 * j + 1], vcat))
        for c in range(n_chunks):
            rows = slice(c * CHUNK, (c + 1) * CHUNK)
            m = mixed[:, c * V7X_LANES:(c + 1) * V7X_LANES] + bias_ref[:, lanes]
            y_ref[rows, lanes] = (u[rows, lanes] * m).astype(_BF16)

    b_gate = zb[:, :D_B]
    s_ref[AB_HALO:AB_HALO + tm, :] = zb[:, D_B:2 * D_B] * zb[:, 2 * D_B:]
    sh = zh[:, :D_B] * zh[:, D_B:]
    s_ref[0:AB_HALO, :] = jnp.where(seq_first, 0.0, sh[:AB_HALO])
    s_ref[AB_HALO + tm:, :] = jnp.where(seq_last, 0.0, sh[AB_HALO:])
    conv = (cw_ref[0:1, :] * s_ref[AB_HALO - 1:AB_HALO - 1 + tm, :]
            + cw_ref[1:2, :] * s_ref[AB_HALO:AB_HALO + tm, :]
            + cw_ref[2:3, :] * s_ref[AB_HALO + 1:AB_HALO + 1 + tm, :])
    y_ref[:, D_A:] = (b_gate * conv).astype(_BF16)

    o_ref[...] = x + _dot(y_ref[...], wout_ref[...])


def _mixer_ab(x, seq_len, g, win, ws, bias, cw, wout):
    n = x.shape[0]
    prev_spec, next_spec = _halo_specs(TM_AB, AB_HALO)
    return pl.pallas_call(
        functools.partial(_mixer_ab_kernel, tiles_per_seq=seq_len // TM_AB),
        grid=(n // TM_AB,),
        in_specs=[_row_tile(TM_AB), prev_spec, next_spec, _resident((1, D_MODEL)),
                  _resident((D_MODEL, 2 * D_A + 3 * D_B)), _resident((A_HEADS, CHUNK, CHUNK)),
                  _resident((CHUNK, D_A)), _resident((B_CONV_W, D_B)), _resident((D_MODEL, D_MODEL))],
        out_specs=_row_tile(TM_AB),
        out_shape=jax.ShapeDtypeStruct((n, D_MODEL), _F32),
        scratch_shapes=[pltpu.VMEM((TM_AB, D_MODEL), _BF16),
                        pltpu.VMEM((TM_AB + 2 * AB_HALO, D_B), _F32)],
        compiler_params=_params(),
        name="mixer_ab",
    )(x, x, x, g, win, ws, bias, cw, wout)


def _conformer_kernel(x_ref, xp_ref, xn_ref, g_ref, w1_ref, b1_ref, dw_ref, dwb_ref, lng_ref, lnb_ref,
                      w2_ref, b2_ref, o_ref, z_ref, p_ref, c_ref, *, tiles_per_seq):
    tm, half_tm = TM_C, TM_C // 2
    n_pair = half_tm + 2 * C_HALO
    i = pl.program_id(0)
    seq_first = (i % tiles_per_seq) == 0
    seq_last = (i % tiles_per_seq) == tiles_per_seq - 1
    x = x_ref[...]
    xe = jnp.concatenate([xp_ref[...], x, xn_ref[...]], axis=0)
    h = _rmsnorm(xe, g_ref[...]).astype(_BF16)
    z = _dot(h, w1_ref[...]) + b1_ref[...]
    glu = z[:, :D_CONV] * jax.nn.sigmoid(z[:, D_CONV:])
    parts = ((0, C_HALO, jnp.where(seq_first, 0.0, glu[:C_HALO])),
             (C_HALO, tm, glu[C_HALO:C_HALO + tm]),
             (C_HALO + tm, C_HALO, jnp.where(seq_last, 0.0, glu[C_HALO + tm:])))
    for start, rows, val in parts:
        for s in range(C_SLABS):
            z_ref[pl.ds(start * C_SLABS + s, rows, stride=C_SLABS), :] = val[:, s * V7X_LANES:(s + 1) * V7X_LANES]
    lower = z_ref[0:n_pair * C_SLABS, :].reshape(n_pair, C_SLABS, V7X_LANES)
    upper = z_ref[half_tm * C_SLABS:(half_tm + n_pair) * C_SLABS, :].reshape(n_pair, C_SLABS, V7X_LANES)
    pairs = jnp.concatenate([lower, upper], axis=1)
    p_ref[...] = pairs.reshape(n_pair * V7X_BF16_SUBLANES, V7X_LANES).astype(_BF16)

    first_tap = C_HALO - CONV_W // 2
    pt = V7X_BF16_SUBLANES

    def conv_groups(it, carry):
        for gi in range(C_GROUPS_PER_ITER):
            j0 = (it * C_GROUPS_PER_ITER + gi) * C_ACCS
            src = pl.multiple_of((j0 + first_tap) * pt, C_ACCS * pt)
            win = p_ref[pl.ds(src, (C_ACCS + CONV_W - 1) * pt), :]
            acc = [None] * C_ACCS
            for k in range(CONV_W):
                wk = dw_ref[k * pt:(k + 1) * pt, :]
                for r in range(C_ACCS):
                    term = win[(r + k) * pt:(r + k + 1) * pt].astype(_F32) * wk.astype(_F32)
                    acc[r] = term if acc[r] is None else acc[r] + term
            lo = jnp.concatenate([a[:C_SLABS] for a in acc], axis=0)
            hi = jnp.concatenate([a[C_SLABS:] for a in acc], axis=0)
            dst = pl.multiple_of(j0 * C_SLABS, C_ACCS * C_SLABS)
            c_ref[pl.ds(dst, C_ACCS * C_SLABS), :] = lo
            c_ref[pl.ds(dst + half_tm * C_SLABS, C_ACCS * C_SLABS), :] = hi
        return carry

    lax.fori_loop(0, half_tm // C_ACCS // C_GROUPS_PER_ITER, conv_groups, 0)
    conv = jnp.concatenate([c_ref[pl.ds(s, tm, stride=C_SLABS), :] for s in range(C_SLABS)], axis=1)

    conv = conv + dwb_ref[...]
    mu = jnp.mean(conv, axis=-1, keepdims=True)
    cen = conv - mu
    var = jnp.mean(cen * cen, axis=-1, keepdims=True)
    ln = cen * lax.rsqrt(var + LN_EPS) * lng_ref[...] + lnb_ref[...]
    act = (ln * jax.nn.sigmoid(ln)).astype(_BF16)
    o_ref[...] = x + _dot(act, w2_ref[...]) + b2_ref[...]


def _conformer(x, seq_len, g, w1, b1, dw, dwb, lng, lnb, w2, b2):
    n = x.shape[0]
    prev_spec, next_spec = _halo_specs(TM_C, C_HALO)
    vec = _resident((1, D_CONV))
    n_ext = TM_C + 2 * C_HALO
    n_pair = TM_C // 2 + 2 * C_HALO
    return pl.pallas_call(
        functools.partial(_conformer_kernel, tiles_per_seq=seq_len // TM_C),
        grid=(n // TM_C,),
        in_specs=[_row_tile(TM_C), prev_spec, next_spec, _resident((1, D_MODEL)),
                  _resident((D_MODEL, 2 * D_CONV)), _resident((1, 2 * D_CONV)),
                  _resident((CONV_W * V7X_BF16_SUBLANES, V7X_LANES)), vec, vec, vec,
                  _resident((D_CONV, D_MODEL)), _resident((1, D_MODEL))],
        out_specs=_row_tile(TM_C),
        out_shape=jax.ShapeDtypeStruct((n, D_MODEL), _F32),
        scratch_shapes=[pltpu.VMEM((n_ext * C_SLABS, V7X_LANES), _F32),
                        pltpu.VMEM((n_pair * V7X_BF16_SUBLANES, V7X_LANES), _BF16),
                        pltpu.VMEM((TM_C * C_SLABS, V7X_LANES), _F32)],
        compiler_params=_params(),
        name="conformer",
    )(x, x, x, g, w1, b1, dw, dwb, lng, lnb, w2, b2)


def _prep_weights(ffn_norm, ffn_w_gate, ffn_w_up, ffn_w_down, mix_norm, ab_w_in, a_spatial_w, a_spatial_b,
                  b_conv_w, ab_w_out, c_w_pw1, c_b_pw1, c_dw_w, c_dw_b, c_ln_g, c_ln_b, c_w_pw2, c_b_pw2,
                  final_norm):
    dw_tiles = c_dw_w.reshape(-1, CONV_W, C_SLABS, V7X_LANES)
    dw_pairs = jnp.concatenate([dw_tiles, dw_tiles], axis=2).astype(_BF16)
    return dict(
        ffn_norm=ffn_norm, mix_norm=mix_norm, final_norm=final_norm,
        ffn_w_gate=ffn_w_gate.astype(_BF16), ffn_w_up=ffn_w_up.astype(_BF16),
        ffn_w_down=ffn_w_down.astype(_BF16),
        ab_w_in=ab_w_in.astype(_BF16), a_spatial_w=a_spatial_w.astype(_BF16),
        a_bias=jnp.repeat(jnp.swapaxes(a_spatial_b, 1, 2), A_HEAD_DIM, axis=2),
        b_conv_w=b_conv_w, ab_w_out=ab_w_out.astype(_BF16),
        c_w_pw1=c_w_pw1.astype(_BF16), c_b_pw1=c_b_pw1,
        c_dw_w=dw_pairs.reshape(-1, CONV_W * V7X_BF16_SUBLANES, V7X_LANES),
        c_dw_b=c_dw_b, c_ln_g=c_ln_g, c_ln_b=c_ln_b, c_w_pw2=c_w_pw2.astype(_BF16), c_b_pw2=c_b_pw2,
    )


def _trunk(x, w):
    batch, seq_len, _ = x.shape
    assert seq_len % TM_FFN == 0 and seq_len % TM_AB == 0 and seq_len % TM_C == 0 and TM_AB % CHUNK == 0
    x = x.reshape(batch * seq_len, D_MODEL)
    depth = w["ffn_norm"].shape[0]
    for l in range(depth):
        x = _ffn(x, w["ffn_norm"][l, 0][None], w["ffn_w_gate"][l, 0], w["ffn_w_up"][l, 0],
                 w["ffn_w_down"][l, 0], w["final_norm"][None], final_norm=False)
        g = w["mix_norm"][l][None]
        i = l // 2
        if l % 2 == 0:
            x = _mixer_ab(x, seq_len, g, w["ab_w_in"][i], w["a_spatial_w"][i], w["a_bias"][i],
                          w["b_conv_w"][i], w["ab_w_out"][i])
        else:
            x = _conformer(x, seq_len, g, w["c_w_pw1"][i], w["c_b_pw1"][i][None], w["c_dw_w"][i],
                           w["c_dw_b"][i][None], w["c_ln_g"][i][None], w["c_ln_b"][i][None],
                           w["c_w_pw2"][i], w["c_b_pw2"][i][None])
        x = _ffn(x, w["ffn_norm"][l, 1][None], w["ffn_w_gate"][l, 1], w["ffn_w_up"][l, 1],
                 w["ffn_w_down"][l, 1], w["final_norm"][None], final_norm=(l == depth - 1))
    return x.reshape(batch, seq_len, D_MODEL)


def kernel(x_prompt, x_sample, ffn_norm, ffn_w_gate, ffn_w_up, ffn_w_down, mix_norm, ab_w_in, a_spatial_w, a_spatial_b, b_conv_w, ab_w_out, c_w_pw1, c_b_pw1, c_dw_w, c_dw_b, c_ln_g, c_ln_b, c_w_pw2, c_b_pw2, final_norm):
    w = _prep_weights(ffn_norm, ffn_w_gate, ffn_w_up, ffn_w_down, mix_norm, ab_w_in, a_spatial_w, a_spatial_b,
                      b_conv_w, ab_w_out, c_w_pw1, c_b_pw1, c_dw_w, c_dw_b, c_ln_g, c_ln_b, c_w_pw2, c_b_pw2,
                      final_norm)
    return (_trunk(x_prompt, w), _trunk(x_sample, w))
```

```python
import functools

import jax
import jax.numpy as jnp
from jax import lax
from jax.experimental import pallas as pl
from jax.experimental.pallas import tpu as pltpu

D_MODEL = 1024
D_FF = 2816
D_A = 512
D_B = 512
A_HEADS = 8
A_HEAD_DIM = 64
CHUNK = 128
B_CONV_W = 3
D_CONV = 1024
CONV_W = 31
RMS_EPS = 1e-6
LN_EPS = 1e-5

V7X_LANES = 128
V7X_F32_SUBLANES = 8
V7X_BF16_SUBLANES = 16
V7X_MXU_DIM = 256
V7X_VMEM_LIMIT_BYTES = 56 * 1024 * 1024

TM_FFN = 1024
TM_AB = 1024
TM_C = 1024
FF_CHUNK = V7X_MXU_DIM
FFN_ROW_BLOCKS = 4
AB_HALO = V7X_F32_SUBLANES
C_HALO = 16
C_SLABS = D_CONV // V7X_LANES
C_ACCS = 4
C_GROUPS_PER_ITER = 16

_BF16 = jnp.bfloat16
_F32 = jnp.float32


def _rmsnorm(x, g):
    ms = jnp.mean(x * x, axis=-1, keepdims=True)
    return x * lax.rsqrt(ms + RMS_EPS) * g


def _dot(a, b):
    return jnp.dot(a, b, preferred_element_type=_F32)


def _resident(shape):
    zeros = (0,) * len(shape)
    return pl.BlockSpec(shape, lambda i: zeros, pipeline_mode=pl.Buffered(1))


def _resident_at(full_shape, lead):
    tail = tuple(full_shape[len(lead):])
    index = tuple(lead) + (0,) * len(tail)
    return pl.BlockSpec((None,) * len(lead) + tail, lambda i: index, pipeline_mode=pl.Buffered(1))


def _row_tile(tm):
    return pl.BlockSpec((tm, D_MODEL), lambda i: (i, 0))


def _halo_specs(tm, halo):
    per_tile = tm // halo

    def prev_map(i):
        return (jnp.maximum(i * per_tile - 1, 0), 0)

    def next_map(i):
        return (jnp.minimum((i + 1) * per_tile, pl.num_programs(0) * per_tile - 1), 0)

    return (pl.BlockSpec((halo, D_MODEL), prev_map), pl.BlockSpec((halo, D_MODEL), next_map))


def _params():
    return pltpu.CompilerParams(dimension_semantics=("parallel",),
                                vmem_limit_bytes=V7X_VMEM_LIMIT_BYTES)


def _ffn_kernel(x_ref, g_ref, wg_ref, wu_ref, wd_ref, fg_ref, o_ref, h_ref, a_ref, *, final_norm):
    blk = TM_FFN // FFN_ROW_BLOCKS
    row_blocks = [slice(r * blk, (r + 1) * blk) for r in range(FFN_ROW_BLOCKS)]

    def act(h, cols):
        gate = _dot(h, wg_ref[:, cols])
        up = _dot(h, wu_ref[:, cols])
        return (gate * jax.nn.sigmoid(gate) * up).astype(_BF16)

    cols0 = slice(0, FF_CHUNK)
    for rows in row_blocks:
        h = _rmsnorm(x_ref[rows, :], g_ref[...]).astype(_BF16)
        h_ref[rows, :] = h
        a_ref[rows, cols0] = act(h, cols0)
    for c in range(1, D_FF // FF_CHUNK):
        cols = slice(c * FF_CHUNK, (c + 1) * FF_CHUNK)
        a_ref[:, cols] = act(h_ref[...], cols)
    for rows in row_blocks:
        y = x_ref[rows, :] + 0.5 * _dot(a_ref[rows, :], wd_ref[...])
        if final_norm:
            y = _rmsnorm(y, fg_ref[...])
        o_ref[rows, :] = y


def _ffn(x, w, l, j, final_norm):
    n = x.shape[0]
    at = (l, j)
    return pl.pallas_call(
        functools.partial(_ffn_kernel, final_norm=final_norm),
        grid=(n // TM_FFN,),
        in_specs=[_row_tile(TM_FFN), _resident_at(w["ffn_norm"].shape, at),
                  _resident_at(w["ffn_w_gate"].shape, at), _resident_at(w["ffn_w_up"].shape, at),
                  _resident_at(w["ffn_w_down"].shape, at), _resident((1, D_MODEL))],
        out_specs=_row_tile(TM_FFN),
        out_shape=jax.ShapeDtypeStruct((n, D_MODEL), _F32),
        scratch_shapes=[pltpu.VMEM((TM_FFN, D_MODEL), _BF16), pltpu.VMEM((TM_FFN, D_FF), _BF16)],
        compiler_params=_params(),
        name="ffn_final" if final_norm else "ffn",
    )(x, w["ffn_norm"], w["ffn_w_gate"], w["ffn_w_up"], w["ffn_w_down"], w["final_norm"])


def _mixer_ab_kernel(x_ref, xp_ref, xn_ref, g_ref, win_ref, ws_ref, bias_ref, cw_ref, wout_ref,
                     o_ref, y_ref, s_ref, *, tiles_per_seq):
    tm = TM_AB
    i = pl.program_id(0)
    seq_first = (i % tiles_per_seq) == 0
    seq_last = (i % tiles_per_seq) == tiles_per_seq - 1
    x = x_ref[...]
    xe = jnp.concatenate([xp_ref[...], x, xn_ref[...]], axis=0)
    h = _rmsnorm(xe, g_ref[...]).astype(_BF16)

    za = _dot(h[AB_HALO:AB_HALO + tm], win_ref[:, :2 * D_A])
    b_gate = _dot(h[AB_HALO:AB_HALO + tm], win_ref[:, 2 * D_A:2 * D_A + D_B])
    zc = _dot(h, win_ref[:, 2 * D_A + D_B:])

    za = jax.nn.gelu(za)
    u = za[:, :D_A]
    v = za[:, D_A:].astype(_BF16)
    n_chunks = tm // CHUNK
    lane = lax.broadcasted_iota(jnp.int32, (CHUNK, n_chunks * V7X_LANES), 1)
    low_half = (lane % V7X_LANES) < A_HEAD_DIM
    for j in range(D_A // V7X_LANES):
        lanes = slice(j * V7X_LANES, (j + 1) * V7X_LANES)
        vcat = jnp.concatenate([v[c * CHUNK:(c + 1) * CHUNK, lanes] for c in range(n_chunks)], axis=1)
        mixed = jnp.where(low_half, _dot(ws_ref[2 * j], vcat), _dot(ws_ref[2 * j + 1], vcat))
        for c in range(n_chunks):
            rows = slice(c * CHUNK, (c + 1) * CHUNK)
            m = mixed[:, c * V7X_LANES:(c + 1) * V7X_LANES] + bias_ref[:, lanes]
            y_ref[rows, lanes] = (u[rows, lanes] * m).astype(_BF16)

    sc = zc[:, :D_B] * zc[:, D_B:]
    s_ref[0:AB_HALO, :] = jnp.where(seq_first, 0.0, sc[:AB_HALO])
    s_ref[AB_HALO:AB_HALO + tm, :] = sc[AB_HALO:AB_HALO + tm]
    s_ref[AB_HALO + tm:, :] = jnp.where(seq_last, 0.0, sc[AB_HALO + tm:])
    conv = (cw_ref[0:1, :] * s_ref[AB_HALO - 1:AB_HALO - 1 + tm, :]
            + cw_ref[1:2, :] * s_ref[AB_HALO:AB_HALO + tm, :]
            + cw_ref[2:3, :] * s_ref[AB_HALO + 1:AB_HALO + 1 + tm, :])
    y_ref[:, D_A:] = (b_gate * conv).astype(_BF16)

    o_ref[...] = x + _dot(y_ref[...], wout_ref[...])


def _mixer_ab(x, seq_len, w, l):
    n = x.shape[0]
    prev_spec, next_spec = _halo_specs(TM_AB, AB_HALO)
    names = ("ab_w_in", "a_spatial_w", "a_bias", "b_conv_w", "ab_w_out")
    return pl.pallas_call(
        functools.partial(_mixer_ab_kernel, tiles_per_seq=seq_len // TM_AB),
        grid=(n // TM_AB,),
        in_specs=[_row_tile(TM_AB), prev_spec, next_spec, _resident_at(w["mix_norm"].shape, (l,))]
        + [_resident_at(w[k].shape, (l // 2,)) for k in names],
        out_specs=_row_tile(TM_AB),
        out_shape=jax.ShapeDtypeStruct((n, D_MODEL), _F32),
        scratch_shapes=[pltpu.VMEM((TM_AB, D_MODEL), _BF16),
                        pltpu.VMEM((TM_AB + 2 * AB_HALO, D_B), _F32)],
        compiler_params=_params(),
        name="mixer_ab",
    )(x, x, x, w["mix_norm"], *[w[k] for k in names])


def _conformer_kernel(x_ref, xp_ref, xn_ref, g_ref, w1_ref, b1_ref, dw_ref, dwb_ref, lng_ref, lnb_ref,
                      w2_ref, b2_ref, o_ref, z_ref, p_ref, c_ref, *, tiles_per_seq):
    tm, half_tm = TM_C, TM_C // 2
    n_pair = half_tm + 2 * C_HALO
    i = pl.program_id(0)
    seq_first = (i % tiles_per_seq) == 0
    seq_last = (i % tiles_per_seq) == tiles_per_seq - 1
    x = x_ref[...]
    xe = jnp.concatenate([xp_ref[...], x, xn_ref[...]], axis=0)
    h = _rmsnorm(xe, g_ref[...]).astype(_BF16)
    z = _dot(h, w1_ref[...]) + b1_ref[...]
    glu = z[:, :D_CONV] * jax.nn.sigmoid(z[:, D_CONV:])
    parts = ((0, C_HALO, jnp.where(seq_first, 0.0, glu[:C_HALO])),
             (C_HALO, tm, glu[C_HALO:C_HALO + tm]),
             (C_HALO + tm, C_HALO, jnp.where(seq_last, 0.0, glu[C_HALO + tm:])))
    for start, rows, val in parts:
        for s in range(C_SLABS):
            z_ref[pl.ds(start * C_SLABS + s, rows, stride=C_SLABS), :] = val[:, s * V7X_LANES:(s + 1) * V7X_LANES]
    lower = z_ref[0:n_pair * C_SLABS, :].reshape(n_pair, C_SLABS, V7X_LANES)
    upper = z_ref[half_tm * C_SLABS:(half_tm + n_pair) * C_SLABS, :].reshape(n_pair, C_SLABS, V7X_LANES)
    pairs = jnp.concatenate([lower, upper], axis=1)
    p_ref[...] = pairs.reshape(n_pair * V7X_BF16_SUBLANES, V7X_LANES).astype(_BF16)

    first_tap = C_HALO - CONV_W // 2
    pt = V7X_BF16_SUBLANES

    def conv_groups(it, carry):
        for gi in range(C_GROUPS_PER_ITER):
            j0 = (it * C_GROUPS_PER_ITER + gi) * C_ACCS
            src = pl.multiple_of((j0 + first_tap) * pt, C_ACCS * pt)
            win = p_ref[pl.ds(src, (C_ACCS + CONV_W - 1) * pt), :]
            acc = [None] * C_ACCS
            for k in range(CONV_W):
                wk = dw_ref[k * pt:(k + 1) * pt, :]
                for r in range(C_ACCS):
                    term = win[(r + k) * pt:(r + k + 1) * pt].astype(_F32) * wk.astype(_F32)
                    acc[r] = term if acc[r] is None else acc[r] + term
            lo = jnp.concatenate([a[:C_SLABS] for a in acc], axis=0)
            hi = jnp.concatenate([a[C_SLABS:] for a in acc], axis=0)
            dst = pl.multiple_of(j0 * C_SLABS, C_ACCS * C_SLABS)
            c_ref[pl.ds(dst, C_ACCS * C_SLABS), :] = lo
            c_ref[pl.ds(dst + half_tm * C_SLABS, C_ACCS * C_SLABS), :] = hi
        return carry

    lax.fori_loop(0, half_tm // C_ACCS // C_GROUPS_PER_ITER, conv_groups, 0)
    conv = jnp.concatenate([c_ref[pl.ds(s, tm, stride=C_SLABS), :] for s in range(C_SLABS)], axis=1)

    conv = conv + dwb_ref[...]
    mu = jnp.mean(conv, axis=-1, keepdims=True)
    cen = conv - mu
    var = jnp.mean(cen * cen, axis=-1, keepdims=True)
    ln = cen * lax.rsqrt(var + LN_EPS) * lng_ref[...] + lnb_ref[...]
    act = (ln * jax.nn.sigmoid(ln)).astype(_BF16)
    o_ref[...] = x + _dot(act, w2_ref[...]) + b2_ref[...]


def _conformer(x, seq_len, w, l):
    n = x.shape[0]
    prev_spec, next_spec = _halo_specs(TM_C, C_HALO)
    n_ext = TM_C + 2 * C_HALO
    n_pair = TM_C // 2 + 2 * C_HALO
    names = ("c_w_pw1", "c_b_pw1", "c_dw_w", "c_dw_b", "c_ln_g", "c_ln_b", "c_w_pw2", "c_b_pw2")
    return pl.pallas_call(
        functools.partial(_conformer_kernel, tiles_per_seq=seq_len // TM_C),
        grid=(n // TM_C,),
        in_specs=[_row_tile(TM_C), prev_spec, next_spec, _resident_at(w["mix_norm"].shape, (l,))]
        + [_resident_at(w[k].shape, (l // 2,)) for k in names],
        out_specs=_row_tile(TM_C),
        out_shape=jax.ShapeDtypeStruct((n, D_MODEL), _F32),
        scratch_shapes=[pltpu.VMEM((n_ext * C_SLABS, V7X_LANES), _F32),
                        pltpu.VMEM((n_pair * V7X_BF16_SUBLANES, V7X_LANES), _BF16),
                        pltpu.VMEM((TM_C * C_SLABS, V7X_LANES), _F32)],
        compiler_params=_params(),
        name="conformer",
    )(x, x, x, w["mix_norm"], *[w[k] for k in names])


def _prep_weights(ffn_norm, ffn_w_gate, ffn_w_up, ffn_w_down, mix_norm, ab_w_in, a_spatial_w, a_spatial_b,
                  b_conv_w, ab_w_out, c_w_pw1, c_b_pw1, c_dw_w, c_dw_b, c_ln_g, c_ln_b, c_w_pw2, c_b_pw2,
                  final_norm):
    dw_tiles = c_dw_w.reshape(-1, CONV_W, C_SLABS, V7X_LANES)
    dw_pairs = jnp.concatenate([dw_tiles, dw_tiles], axis=2).astype(_BF16)
    row = lambda v: v[..., None, :]
    return dict(
        ffn_norm=row(ffn_norm), mix_norm=row(mix_norm), final_norm=row(final_norm),
        ffn_w_gate=ffn_w_gate.astype(_BF16), ffn_w_up=ffn_w_up.astype(_BF16),
        ffn_w_down=ffn_w_down.astype(_BF16),
        ab_w_in=ab_w_in.astype(_BF16), a_spatial_w=a_spatial_w.astype(_BF16),
        a_bias=jnp.repeat(jnp.swapaxes(a_spatial_b, 1, 2), A_HEAD_DIM, axis=2),
        b_conv_w=b_conv_w, ab_w_out=ab_w_out.astype(_BF16),
        c_w_pw1=c_w_pw1.astype(_BF16), c_b_pw1=row(c_b_pw1),
        c_dw_w=dw_pairs.reshape(-1, CONV_W * V7X_BF16_SUBLANES, V7X_LANES),
        c_dw_b=row(c_dw_b), c_ln_g=row(c_ln_g), c_ln_b=row(c_ln_b),
        c_w_pw2=c_w_pw2.astype(_BF16), c_b_pw2=row(c_b_pw2),
    )


def _trunk(x, w):
    batch, seq_len, _ = x.shape
    assert seq_len % TM_FFN == 0 and seq_len % TM_AB == 0 and seq_len % TM_C == 0 and TM_AB % CHUNK == 0
    x = x.reshape(batch * seq_len, D_MODEL)
    depth = w["ffn_norm"].shape[0]
    for l in range(depth):
        x = _ffn(x, w, l, 0, final_norm=False)
        x = _mixer_ab(x, seq_len, w, l) if l % 2 == 0 else _conformer(x, seq_len, w, l)
        x = _ffn(x, w, l, 1, final_norm=(l == depth - 1))
    return x.reshape(batch, seq_len, D_MODEL)


def kernel(x_prompt, x_sample, ffn_norm, ffn_w_gate, ffn_w_up, ffn_w_down, mix_norm, ab_w_in, a_spatial_w, a_spatial_b, b_conv_w, ab_w_out, c_w_pw1, c_b_pw1, c_dw_w, c_dw_b, c_ln_g, c_ln_b, c_w_pw2, c_b_pw2, final_norm):
    w = _prep_weights(ffn_norm, ffn_w_gate, ffn_w_up, ffn_w_down, mix_norm, ab_w_in, a_spatial_w, a_spatial_b,
                      b_conv_w, ab_w_out, c_w_pw1, c_b_pw1, c_dw_w, c_dw_b, c_ln_g, c_ln_b, c_w_pw2, c_b_pw2,
                      final_norm)
    return (_trunk(x_prompt, w), _trunk(x_sample, w))
```

```python
import functools

import jax
import jax.numpy as jnp
from jax import lax
from jax.experimental import pallas as pl
from jax.experimental.pallas import tpu as pltpu

D_MODEL = 1024
D_FF = 2816
D_A = 512
D_B = 512
A_HEADS = 8
A_HEAD_DIM = 64
CHUNK = 128
B_CONV_W = 3
D_CONV = 1024
CONV_W = 31
RMS_EPS = 1e-6
LN_EPS = 1e-5

V7X_LANES = 128
V7X_F32_SUBLANES = 8
V7X_BF16_SUBLANES = 16
V7X_MXU_DIM = 256
V7X_VMEM_LIMIT_BYTES = 56 * 1024 * 1024

TM_FFN = 1024
TM_AB = 1024
TM_C = 1024
FF_CHUNK = V7X_MXU_DIM
FFN_ROW_BLOCKS = 4
FFN_DOWN_ROW_BLOCKS = 2
AB_HALO = V7X_F32_SUBLANES
C_HALO = 16
C_SLABS = D_CONV // V7X_LANES
C_ACCS = 4
C_GROUPS_PER_ITER = 16

_BF16 = jnp.bfloat16
_F32 = jnp.float32


def _rmsnorm(x, g):
    ms = jnp.mean(x * x, axis=-1, keepdims=True)
    return x * lax.rsqrt(ms + RMS_EPS) * g


def _dot(a, b):
    return jnp.dot(a, b, preferred_element_type=_F32)


def _resident(shape):
    zeros = (0,) * len(shape)
    return pl.BlockSpec(shape, lambda i: zeros, pipeline_mode=pl.Buffered(1))


def _resident_at(full_shape, lead):
    tail = tuple(full_shape[len(lead):])
    index = tuple(lead) + (0,) * len(tail)
    return pl.BlockSpec((None,) * len(lead) + tail, lambda i: index, pipeline_mode=pl.Buffered(1))


def _row_tile(tm):
    return pl.BlockSpec((tm, D_MODEL), lambda i: (i, 0))


def _halo_specs(tm, halo):
    per_tile = tm // halo

    def prev_map(i):
        return (jnp.maximum(i * per_tile - 1, 0), 0)

    def next_map(i):
        return (jnp.minimum((i + 1) * per_tile, pl.num_programs(0) * per_tile - 1), 0)

    return (pl.BlockSpec((halo, D_MODEL), prev_map), pl.BlockSpec((halo, D_MODEL), next_map))


def _params():
    return pltpu.CompilerParams(dimension_semantics=("parallel",),
                                vmem_limit_bytes=V7X_VMEM_LIMIT_BYTES)


def _ffn_kernel(x_ref, g_ref, wg_ref, wu_ref, wd_ref, fg_ref, o_ref, h_ref, a_ref, *, final_norm):
    blk = TM_FFN // FFN_ROW_BLOCKS
    row_blocks = [slice(r * blk, (r + 1) * blk) for r in range(FFN_ROW_BLOCKS)]

    def act(h, cols):
        gate = _dot(h, wg_ref[:, cols])
        up = _dot(h, wu_ref[:, cols])
        return (gate * jax.nn.sigmoid(gate) * up).astype(_BF16)

    cols0 = slice(0, FF_CHUNK)
    for rows in row_blocks:
        h = _rmsnorm(x_ref[rows, :], g_ref[...]).astype(_BF16)
        h_ref[rows, :] = h
        a_ref[rows, cols0] = act(h, cols0)
    for c in range(1, D_FF // FF_CHUNK):
        cols = slice(c * FF_CHUNK, (c + 1) * FF_CHUNK)
        a_ref[:, cols] = act(h_ref[...], cols)
    dblk = TM_FFN // FFN_DOWN_ROW_BLOCKS
    for rows in [slice(r * dblk, (r + 1) * dblk) for r in range(FFN_DOWN_ROW_BLOCKS)]:
        y = x_ref[rows, :] + 0.5 * _dot(a_ref[rows, :], wd_ref[...])
        if final_norm:
            y = _rmsnorm(y, fg_ref[...])
        o_ref[rows, :] = y


def _ffn(x, w, l, j, final_norm):
    n = x.shape[0]
    at = (l, j)
    return pl.pallas_call(
        functools.partial(_ffn_kernel, final_norm=final_norm),
        grid=(n // TM_FFN,),
        in_specs=[_row_tile(TM_FFN), _resident_at(w["ffn_norm"].shape, at),
                  _resident_at(w["ffn_w_gate"].shape, at), _resident_at(w["ffn_w_up"].shape, at),
                  _resident_at(w["ffn_w_down"].shape, at), _resident((1, D_MODEL))],
        out_specs=_row_tile(TM_FFN),
        out_shape=jax.ShapeDtypeStruct((n, D_MODEL), _F32),
        scratch_shapes=[pltpu.VMEM((TM_FFN, D_MODEL), _BF16), pltpu.VMEM((TM_FFN, D_FF), _BF16)],
        compiler_params=_params(),
        name="ffn_final" if final_norm else "ffn",
    )(x, w["ffn_norm"], w["ffn_w_gate"], w["ffn_w_up"], w["ffn_w_down"], w["final_norm"])


def _mixer_ab_kernel(x_ref, xp_ref, xn_ref, g_ref, win_ref, ws_ref, bias_ref, cw_ref, wout_ref,
                     o_ref, y_ref, s_ref, *, tiles_per_seq):
    tm = TM_AB
    i = pl.program_id(0)
    seq_first = (i % tiles_per_seq) == 0
    seq_last = (i % tiles_per_seq) == tiles_per_seq - 1
    x = x_ref[...]
    xe = jnp.concatenate([xp_ref[...], x, xn_ref[...]], axis=0)
    h = _rmsnorm(xe, g_ref[...]).astype(_BF16)

    za = _dot(h[AB_HALO:AB_HALO + tm], win_ref[:, :2 * D_A])
    b_gate = _dot(h[AB_HALO:AB_HALO + tm], win_ref[:, 2 * D_A:2 * D_A + D_B])
    zc = _dot(h, win_ref[:, 2 * D_A + D_B:])

    za = jax.nn.gelu(za)
    u = za[:, :D_A]
    v = za[:, D_A:].astype(_BF16)
    n_chunks = tm // CHUNK
    lane = lax.broadcasted_iota(jnp.int32, (CHUNK, n_chunks * V7X_LANES), 1)
    low_half = (lane % V7X_LANES) < A_HEAD_DIM
    for j in range(D_A // V7X_LANES):
        lanes = slice(j * V7X_LANES, (j + 1) * V7X_LANES)
        vcat = jnp.concatenate([v[c * CHUNK:(c + 1) * CHUNK, lanes] for c in range(n_chunks)], axis=1)
        mixed = jnp.where(low_half, _dot(ws_ref[2 * j], vcat), _dot(ws_ref[2 * j + 1], vcat))
        for c in range(n_chunks):
            rows = slice(c * CHUNK, (c + 1) * CHUNK)
            m = mixed[:, c * V7X_LANES:(c + 1) * V7X_LANES] + bias_ref[:, lanes]
            y_ref[rows, lanes] = (u[rows, lanes] * m).astype(_BF16)

    sc = zc[:, :D_B] * zc[:, D_B:]
    s_ref[0:AB_HALO, :] = jnp.where(seq_first, 0.0, sc[:AB_HALO])
    s_ref[AB_HALO:AB_HALO + tm, :] = sc[AB_HALO:AB_HALO + tm]
    s_ref[AB_HALO + tm:, :] = jnp.where(seq_last, 0.0, sc[AB_HALO + tm:])
    conv = (cw_ref[0:1, :] * s_ref[AB_HALO - 1:AB_HALO - 1 + tm, :]
            + cw_ref[1:2, :] * s_ref[AB_HALO:AB_HALO + tm, :]
            + cw_ref[2:3, :] * s_ref[AB_HALO + 1:AB_HALO + 1 + tm, :])
    y_ref[:, D_A:] = (b_gate * conv).astype(_BF16)

    o_ref[...] = x + _dot(y_ref[...], wout_ref[...])


def _mixer_ab(x, seq_len, w, l):
    n = x.shape[0]
    prev_spec, next_spec = _halo_specs(TM_AB, AB_HALO)
    names = ("ab_w_in", "a_spatial_w", "a_bias", "b_conv_w", "ab_w_out")
    return pl.pallas_call(
        functools.partial(_mixer_ab_kernel, tiles_per_seq=seq_len // TM_AB),
        grid=(n // TM_AB,),
        in_specs=[_row_tile(TM_AB), prev_spec, next_spec, _resident_at(w["mix_norm"].shape, (l,))]
        + [_resident_at(w[k].shape, (l // 2,)) for k in names],
        out_specs=_row_tile(TM_AB),
        out_shape=jax.ShapeDtypeStruct((n, D_MODEL), _F32),
        scratch_shapes=[pltpu.VMEM((TM_AB, D_MODEL), _BF16),
                        pltpu.VMEM((TM_AB + 2 * AB_HALO, D_B), _F32)],
        compiler_params=_params(),
        name="mixer_ab",
    )(x, x, x, w["mix_norm"], *[w[k] for k in names])


def _conformer_kernel(x_ref, xp_ref, xn_ref, g_ref, w1_ref, b1_ref, dw_ref, dwb_ref, lng_ref, lnb_ref,
                      w2_ref, b2_ref, o_ref, z_ref, p_ref, c_ref, *, tiles_per_seq):
    tm, half_tm = TM_C, TM_C // 2
    n_pair = half_tm + 2 * C_HALO
    i = pl.program_id(0)
    seq_first = (i % tiles_per_seq) == 0
    seq_last = (i % tiles_per_seq) == tiles_per_seq - 1
    x = x_ref[...]
    xe = jnp.concatenate([xp_ref[...], x, xn_ref[...]], axis=0)
    h = _rmsnorm(xe, g_ref[...]).astype(_BF16)
    z = _dot(h, w1_ref[...]) + b1_ref[...]
    glu = z[:, :D_CONV] * jax.nn.sigmoid(z[:, D_CONV:])
    parts = ((0, C_HALO, jnp.where(seq_first, 0.0, glu[:C_HALO])),
             (C_HALO, tm, glu[C_HALO:C_HALO + tm]),
             (C_HALO + tm, C_HALO, jnp.where(seq_last, 0.0, glu[C_HALO + tm:])))
    for start, rows, val in parts:
        for s in range(C_SLABS):
            z_ref[pl.ds(start * C_SLABS + s, rows, stride=C_SLABS), :] = val[:, s * V7X_LANES:(s + 1) * V7X_LANES]
    lower = z_ref[0:n_pair * C_SLABS, :].reshape(n_pair, C_SLABS, V7X_LANES)
    upper = z_ref[half_tm * C_SLABS:(half_tm + n_pair) * C_SLABS, :].reshape(n_pair, C_SLABS, V7X_LANES)
    pairs = jnp.concatenate([lower, upper], axis=1)
    p_ref[...] = pairs.reshape(n_pair * V7X_BF16_SUBLANES, V7X_LANES).astype(_BF16)

    first_tap = C_HALO - CONV_W // 2
    pt = V7X_BF16_SUBLANES

    def conv_groups(it, carry):
        for gi in range(C_GROUPS_PER_ITER):
            j0 = (it * C_GROUPS_PER_ITER + gi) * C_ACCS
            src = pl.multiple_of((j0 + first_tap) * pt, C_ACCS * pt)
            win = p_ref[pl.ds(src, (C_ACCS + CONV_W - 1) * pt), :]
            acc = [None] * C_ACCS
            for k in range(CONV_W):
                wk = dw_ref[k * pt:(k + 1) * pt, :]
                for r in range(C_ACCS):
                    term = win[(r + k) * pt:(r + k + 1) * pt].astype(_F32) * wk.astype(_F32)
                    acc[r] = term if acc[r] is None else acc[r] + term
            lo = jnp.concatenate([a[:C_SLABS] for a in acc], axis=0)
            hi = jnp.concatenate([a[C_SLABS:] for a in acc], axis=0)
            dst = pl.multiple_of(j0 * C_SLABS, C_ACCS * C_SLABS)
            c_ref[pl.ds(dst, C_ACCS * C_SLABS), :] = lo
            c_ref[pl.ds(dst + half_tm * C_SLABS, C_ACCS * C_SLABS), :] = hi
        return carry

    lax.fori_loop(0, half_tm // C_ACCS // C_GROUPS_PER_ITER, conv_groups, 0)
    conv = jnp.concatenate([c_ref[pl.ds(s, tm, stride=C_SLABS), :] for s in range(C_SLABS)], axis=1)

    conv = conv + dwb_ref[...]
    mu = jnp.mean(conv, axis=-1, keepdims=True)
    cen = conv - mu
    var = jnp.mean(cen * cen, axis=-1, keepdims=True)
    ln = cen * lax.rsqrt(var + LN_EPS) * lng_ref[...] + lnb_ref[...]
    act = (ln * jax.nn.sigmoid(ln)).astype(_BF16)
    o_ref[...] = x + _dot(act, w2_ref[...]) + b2_ref[...]


def _conformer(x, seq_len, w, l):
    n = x.shape[0]
    prev_spec, next_spec = _halo_specs(TM_C, C_HALO)
    n_ext = TM_C + 2 * C_HALO
    n_pair = TM_C // 2 + 2 * C_HALO
    names = ("c_w_pw1", "c_b_pw1", "c_dw_w", "c_dw_b", "c_ln_g", "c_ln_b", "c_w_pw2", "c_b_pw2")
    return pl.pallas_call(
        functools.partial(_conformer_kernel, tiles_per_seq=seq_len // TM_C),
        grid=(n // TM_C,),
        in_specs=[_row_tile(TM_C), prev_spec, next_spec, _resident_at(w["mix_norm"].shape, (l,))]
        + [_resident_at(w[k].shape, (l // 2,)) for k in names],
        out_specs=_row_tile(TM_C),
        out_shape=jax.ShapeDtypeStruct((n, D_MODEL), _F32),
        scratch_shapes=[pltpu.VMEM((n_ext * C_SLABS, V7X_LANES), _F32),
                        pltpu.VMEM((n_pair * V7X_BF16_SUBLANES, V7X_LANES), _BF16),
                        pltpu.VMEM((TM_C * C_SLABS, V7X_LANES), _F32)],
        compiler_params=_params(),
        name="conformer",
    )(x, x, x, w["mix_norm"], *[w[k] for k in names])


def _prep_weights(ffn_norm, ffn_w_gate, ffn_w_up, ffn_w_down, mix_norm, ab_w_in, a_spatial_w, a_spatial_b,
                  b_conv_w, ab_w_out, c_w_pw1, c_b_pw1, c_dw_w, c_dw_b, c_ln_g, c_ln_b, c_w_pw2, c_b_pw2,
                  final_norm):
    dw_tiles = c_dw_w.reshape(-1, CONV_W, C_SLABS, V7X_LANES)
    dw_pairs = jnp.concatenate([dw_tiles, dw_tiles], axis=2).astype(_BF16)
    row = lambda v: v[..., None, :]
    return dict(
        ffn_norm=row(ffn_norm), mix_norm=row(mix_norm), final_norm=row(final_norm),
        ffn_w_gate=ffn_w_gate.astype(_BF16), ffn_w_up=ffn_w_up.astype(_BF16),
        ffn_w_down=ffn_w_down.astype(_BF16),
        ab_w_in=ab_w_in.astype(_BF16), a_spatial_w=a_spatial_w.astype(_BF16),
        a_bias=jnp.repeat(jnp.swapaxes(a_spatial_b, 1, 2), A_HEAD_DIM, axis=2),
        b_conv_w=b_conv_w, ab_w_out=ab_w_out.astype(_BF16),
        c_w_pw1=c_w_pw1.astype(_BF16), c_b_pw1=row(c_b_pw1),
        c_dw_w=dw_pairs.reshape(-1, CONV_W * V7X_BF16_SUBLANES, V7X_LANES),
        c_dw_b=row(c_dw_b), c_ln_g=row(c_ln_g), c_ln_b=row(c_ln_b),
        c_w_pw2=c_w_pw2.astype(_BF16), c_b_pw2=row(c_b_pw2),
    )


def _trunk(x, w):
    batch, seq_len, _ = x.shape
    assert seq_len % TM_FFN == 0 and seq_len % TM_AB == 0 and seq_len % TM_C == 0 and TM_AB % CHUNK == 0
    x = x.reshape(batch * seq_len, D_MODEL)
    depth = w["ffn_norm"].shape[0]
    for l in range(depth):
        x = _ffn(x, w, l, 0, final_norm=False)
        x = _mixer_ab(x, seq_len, w, l) if l % 2 == 0 else _conformer(x, seq_len, w, l)
        x = _ffn(x, w, l, 1, final_norm=(l == depth - 1))
    return x.reshape(batch, seq_len, D_MODEL)


def kernel(x_prompt, x_sample, ffn_norm, ffn_w_gate, ffn_w_up, ffn_w_down, mix_norm, ab_w_in, a_spatial_w, a_spatial_b, b_conv_w, ab_w_out, c_w_pw1, c_b_pw1, c_dw_w, c_dw_b, c_ln_g, c_ln_b, c_w_pw2, c_b_pw2, final_norm):
    w = _prep_weights(ffn_norm, ffn_w_gate, ffn_w_up, ffn_w_down, mix_norm, ab_w_in, a_spatial_w, a_spatial_b,
                      b_conv_w, ab_w_out, c_w_pw1, c_b_pw1, c_dw_w, c_dw_b, c_ln_g, c_ln_b, c_w_pw2, c_b_pw2,
                      final_norm)
    return (_trunk(x_prompt, w), _trunk(x_sample, w))
```

```python
import functools

import jax
import jax.numpy as jnp
from jax import lax
from jax.experimental import pallas as pl
from jax.experimental.pallas import tpu as pltpu

D_MODEL = 1024
D_FF = 2816
D_A = 512
D_B = 512
A_HEADS = 8
A_HEAD_DIM = 64
CHUNK = 128
B_CONV_W = 3
D_CONV = 1024
CONV_W = 31
RMS_EPS = 1e-6
LN_EPS = 1e-5

V7X_LANES = 128
V7X_F32_SUBLANES = 8
V7X_BF16_SUBLANES = 16
V7X_MXU_DIM = 256
V7X_VMEM_LIMIT_BYTES = 56 * 1024 * 1024

TM_FFN = 1024
TM_AB = 1024
TM_C = 1024
FF_CHUNK = V7X_MXU_DIM
FFN_ROW_BLOCKS = 4
AB_HALO = V7X_F32_SUBLANES
C_HALO = 16
C_SLABS = D_CONV // V7X_LANES
C_ACCS = 4
C_GROUPS_PER_ITER = 16

_BF16 = jnp.bfloat16
_F32 = jnp.float32


def _rmsnorm(x, g):
    ms = jnp.mean(x * x, axis=-1, keepdims=True)
    return x * lax.rsqrt(ms + RMS_EPS) * g


def _dot(a, b):
    return jnp.dot(a, b, preferred_element_type=_F32)


def _resident(shape):
    zeros = (0,) * len(shape)
    return pl.BlockSpec(shape, lambda i: zeros, pipeline_mode=pl.Buffered(1))


def _resident_at(full_shape, lead):
    tail = tuple(full_shape[len(lead):])
    index = tuple(lead) + (0,) * len(tail)
    return pl.BlockSpec((None,) * len(lead) + tail, lambda i: index, pipeline_mode=pl.Buffered(1))


def _row_tile(tm):
    return pl.BlockSpec((tm, D_MODEL), lambda i: (i, 0))


def _halo_specs(tm, halo):
    per_tile = tm // halo

    def prev_map(i):
        return (jnp.maximum(i * per_tile - 1, 0), 0)

    def next_map(i):
        return (jnp.minimum((i + 1) * per_tile, pl.num_programs(0) * per_tile - 1), 0)

    return (pl.BlockSpec((halo, D_MODEL), prev_map), pl.BlockSpec((halo, D_MODEL), next_map))


def _params():
    return pltpu.CompilerParams(dimension_semantics=("arbitrary",),
                                vmem_limit_bytes=V7X_VMEM_LIMIT_BYTES)


def _ffn_kernel(x_ref, g_ref, wg_ref, wu_ref, wd_ref, fg_ref, o_ref, h_ref, a_ref, *, final_norm):
    blk = TM_FFN // FFN_ROW_BLOCKS
    row_blocks = [slice(r * blk, (r + 1) * blk) for r in range(FFN_ROW_BLOCKS)]

    def act(h, cols):
        gate = _dot(h, wg_ref[:, cols])
        up = _dot(h, wu_ref[:, cols])
        return (gate * jax.nn.sigmoid(gate) * up).astype(_BF16)

    cols0 = slice(0, FF_CHUNK)
    for rows in row_blocks:
        h = _rmsnorm(x_ref[rows, :], g_ref[...]).astype(_BF16)
        h_ref[rows, :] = h
        a_ref[rows, cols0] = act(h, cols0)
    for c in range(1, D_FF // FF_CHUNK):
        cols = slice(c * FF_CHUNK, (c + 1) * FF_CHUNK)
        a_ref[:, cols] = act(h_ref[...], cols)
    for rows in row_blocks:
        y = x_ref[rows, :] + 0.5 * _dot(a_ref[rows, :], wd_ref[...])
        if final_norm:
            y = _rmsnorm(y, fg_ref[...])
        o_ref[rows, :] = y


def _ffn(x, w, l, j, final_norm):
    n = x.shape[0]
    at = (l, j)
    return pl.pallas_call(
        functools.partial(_ffn_kernel, final_norm=final_norm),
        grid=(n // TM_FFN,),
        in_specs=[_row_tile(TM_FFN), _resident_at(w["ffn_norm"].shape, at),
                  _resident_at(w["ffn_w_gate"].shape, at), _resident_at(w["ffn_w_up"].shape, at),
                  _resident_at(w["ffn_w_down"].shape, at), _resident((1, D_MODEL))],
        out_specs=_row_tile(TM_FFN),
        out_shape=jax.ShapeDtypeStruct((n, D_MODEL), _F32),
        scratch_shapes=[pltpu.VMEM((TM_FFN, D_MODEL), _BF16), pltpu.VMEM((TM_FFN, D_FF), _BF16)],
        compiler_params=_params(),
        name="ffn_final" if final_norm else "ffn",
    )(x, w["ffn_norm"], w["ffn_w_gate"], w["ffn_w_up"], w["ffn_w_down"], w["final_norm"])


def _mixer_ab_kernel(x_ref, xp_ref, xn_ref, g_ref, win_ref, ws_ref, bias_ref, cw_ref, wout_ref,
                     o_ref, y_ref, s_ref, *, tiles_per_seq):
    tm = TM_AB
    i = pl.program_id(0)
    seq_first = (i % tiles_per_seq) == 0
    seq_last = (i % tiles_per_seq) == tiles_per_seq - 1
    x = x_ref[...]
    xe = jnp.concatenate([xp_ref[...], x, xn_ref[...]], axis=0)
    h = _rmsnorm(xe, g_ref[...]).astype(_BF16)

    za = _dot(h[AB_HALO:AB_HALO + tm], win_ref[:, :2 * D_A])
    b_gate = _dot(h[AB_HALO:AB_HALO + tm], win_ref[:, 2 * D_A:2 * D_A + D_B])
    zc = _dot(h, win_ref[:, 2 * D_A + D_B:])

    za = jax.nn.gelu(za)
    u = za[:, :D_A]
    v = za[:, D_A:].astype(_BF16)
    n_chunks = tm // CHUNK
    lane = lax.broadcasted_iota(jnp.int32, (CHUNK, n_chunks * V7X_LANES), 1)
    low_half = (lane % V7X_LANES) < A_HEAD_DIM
    for j in range(D_A // V7X_LANES):
        lanes = slice(j * V7X_LANES, (j + 1) * V7X_LANES)
        vcat = jnp.concatenate([v[c * CHUNK:(c + 1) * CHUNK, lanes] for c in range(n_chunks)], axis=1)
        mixed = jnp.where(low_half, _dot(ws_ref[2 * j], vcat), _dot(ws_ref[2 * j + 1], vcat))
        for c in range(n_chunks):
            rows = slice(c * CHUNK, (c + 1) * CHUNK)
            m = mixed[:, c * V7X_LANES:(c + 1) * V7X_LANES] + bias_ref[:, lanes]
            y_ref[rows, lanes] = (u[rows, lanes] * m).astype(_BF16)

    sc = zc[:, :D_B] * zc[:, D_B:]
    s_ref[0:AB_HALO, :] = jnp.where(seq_first, 0.0, sc[:AB_HALO])
    s_ref[AB_HALO:AB_HALO + tm, :] = sc[AB_HALO:AB_HALO + tm]
    s_ref[AB_HALO + tm:, :] = jnp.where(seq_last, 0.0, sc[AB_HALO + tm:])
    conv = (cw_ref[0:1, :] * s_ref[AB_HALO - 1:AB_HALO - 1 + tm, :]
            + cw_ref[1:2, :] * s_ref[AB_HALO:AB_HALO + tm, :]
            + cw_ref[2:3, :] * s_ref[AB_HALO + 1:AB_HALO + 1 + tm, :])
    y_ref[:, D_A:] = (b_gate * conv).astype(_BF16)

    o_ref[...] = x + _dot(y_ref[...], wout_ref[...])


def _mixer_ab(x, seq_len, w, l):
    n = x.shape[0]
    prev_spec, next_spec = _halo_specs(TM_AB, AB_HALO)
    names = ("ab_w_in", "a_spatial_w", "a_bias", "b_conv_w", "ab_w_out")
    return pl.pallas_call(
        functools.partial(_mixer_ab_kernel, tiles_per_seq=seq_len // TM_AB),
        grid=(n // TM_AB,),
        in_specs=[_row_tile(TM_AB), prev_spec, next_spec, _resident_at(w["mix_norm"].shape, (l,))]
        + [_resident_at(w[k].shape, (l // 2,)) for k in names],
        out_specs=_row_tile(TM_AB),
        out_shape=jax.ShapeDtypeStruct((n, D_MODEL), _F32),
        scratch_shapes=[pltpu.VMEM((TM_AB, D_MODEL), _BF16),
                        pltpu.VMEM((TM_AB + 2 * AB_HALO, D_B), _F32)],
        compiler_params=_params(),
        name="mixer_ab",
    )(x, x, x, w["mix_norm"], *[w[k] for k in names])


def _conformer_kernel(x_ref, xp_ref, xn_ref, g_ref, w1_ref, b1_ref, dw_ref, dwb_ref, lng_ref, lnb_ref,
                      w2_ref, b2_ref, o_ref, z_ref, p_ref, c_ref, *, tiles_per_seq):
    tm, half_tm = TM_C, TM_C // 2
    n_pair = half_tm + 2 * C_HALO
    i = pl.program_id(0)
    seq_first = (i % tiles_per_seq) == 0
    seq_last = (i % tiles_per_seq) == tiles_per_seq - 1
    x = x_ref[...]
    xe = jnp.concatenate([xp_ref[...], x, xn_ref[...]], axis=0)
    h = _rmsnorm(xe, g_ref[...]).astype(_BF16)
    z = _dot(h, w1_ref[...]) + b1_ref[...]
    glu = z[:, :D_CONV] * jax.nn.sigmoid(z[:, D_CONV:])
    parts = ((0, C_HALO, jnp.where(seq_first, 0.0, glu[:C_HALO])),
             (C_HALO, tm, glu[C_HALO:C_HALO + tm]),
             (C_HALO + tm, C_HALO, jnp.where(seq_last, 0.0, glu[C_HALO + tm:])))
    for start, rows, val in parts:
        for s in range(C_SLABS):
            z_ref[pl.ds(start * C_SLABS + s, rows, stride=C_SLABS), :] = val[:, s * V7X_LANES:(s + 1) * V7X_LANES]
    lower = z_ref[0:n_pair * C_SLABS, :].reshape(n_pair, C_SLABS, V7X_LANES)
    upper = z_ref[half_tm * C_SLABS:(half_tm + n_pair) * C_SLABS, :].reshape(n_pair, C_SLABS, V7X_LANES)
    pairs = jnp.concatenate([lower, upper], axis=1)
    p_ref[...] = pairs.reshape(n_pair * V7X_BF16_SUBLANES, V7X_LANES).astype(_BF16)

    first_tap = C_HALO - CONV_W // 2
    pt = V7X_BF16_SUBLANES

    def conv_groups(it, carry):
        for gi in range(C_GROUPS_PER_ITER):
            j0 = (it * C_GROUPS_PER_ITER + gi) * C_ACCS
            src = pl.multiple_of((j0 + first_tap) * pt, C_ACCS * pt)
            win = p_ref[pl.ds(src, (C_ACCS + CONV_W - 1) * pt), :]
            acc = [None] * C_ACCS
            for k in range(CONV_W):
                wk = dw_ref[k * pt:(k + 1) * pt, :]
                for r in range(C_ACCS):
                    term = win[(r + k) * pt:(r + k + 1) * pt].astype(_F32) * wk.astype(_F32)
                    acc[r] = term if acc[r] is None else acc[r] + term
            lo = jnp.concatenate([a[:C_SLABS] for a in acc], axis=0)
            hi = jnp.concatenate([a[C_SLABS:] for a in acc], axis=0)
            dst = pl.multiple_of(j0 * C_SLABS, C_ACCS * C_SLABS)
            c_ref[pl.ds(dst, C_ACCS * C_SLABS), :] = lo
            c_ref[pl.ds(dst + half_tm * C_SLABS, C_ACCS * C_SLABS), :] = hi
        return carry

    lax.fori_loop(0, half_tm // C_ACCS // C_GROUPS_PER_ITER, conv_groups, 0)
    conv = jnp.concatenate([c_ref[pl.ds(s, tm, stride=C_SLABS), :] for s in range(C_SLABS)], axis=1)

    conv = conv + dwb_ref[...]
    mu = jnp.mean(conv, axis=-1, keepdims=True)
    cen = conv - mu
    var = jnp.mean(cen * cen, axis=-1, keepdims=True)
    ln = cen * lax.rsqrt(var + LN_EPS) * lng_ref[...] + lnb_ref[...]
    act = (ln * jax.nn.sigmoid(ln)).astype(_BF16)
    o_ref[...] = x + _dot(act, w2_ref[...]) + b2_ref[...]


def _conformer(x, seq_len, w, l):
    n = x.shape[0]
    prev_spec, next_spec = _halo_specs(TM_C, C_HALO)
    n_ext = TM_C + 2 * C_HALO
    n_pair = TM_C // 2 + 2 * C_HALO
    names = ("c_w_pw1", "c_b_pw1", "c_dw_w", "c_dw_b", "c_ln_g", "c_ln_b", "c_w_pw2", "c_b_pw2")
    return pl.pallas_call(
        functools.partial(_conformer_kernel, tiles_per_seq=seq_len // TM_C),
        grid=(n // TM_C,),
        in_specs=[_row_tile(TM_C), prev_spec, next_spec, _resident_at(w["mix_norm"].shape, (l,))]
        + [_resident_at(w[k].shape, (l // 2,)) for k in names],
        out_specs=_row_tile(TM_C),
        out_shape=jax.ShapeDtypeStruct((n, D_MODEL), _F32),
        scratch_shapes=[pltpu.VMEM((n_ext * C_SLABS, V7X_LANES), _F32),
                        pltpu.VMEM((n_pair * V7X_BF16_SUBLANES, V7X_LANES), _BF16),
                        pltpu.VMEM((TM_C * C_SLABS, V7X_LANES), _F32)],
        compiler_params=_params(),
        name="conformer",
    )(x, x, x, w["mix_norm"], *[w[k] for k in names])


def _prep_weights(ffn_norm, ffn_w_gate, ffn_w_up, ffn_w_down, mix_norm, ab_w_in, a_spatial_w, a_spatial_b,
                  b_conv_w, ab_w_out, c_w_pw1, c_b_pw1, c_dw_w, c_dw_b, c_ln_g, c_ln_b, c_w_pw2, c_b_pw2,
                  final_norm):
    dw_tiles = c_dw_w.reshape(-1, CONV_W, C_SLABS, V7X_LANES)
    dw_pairs = jnp.concatenate([dw_tiles, dw_tiles], axis=2).astype(_BF16)
    row = lambda v: v[..., None, :]
    return dict(
        ffn_norm=row(ffn_norm), mix_norm=row(mix_norm), final_norm=row(final_norm),
        ffn_w_gate=ffn_w_gate.astype(_BF16), ffn_w_up=ffn_w_up.astype(_BF16),
        ffn_w_down=ffn_w_down.astype(_BF16),
        ab_w_in=ab_w_in.astype(_BF16), a_spatial_w=a_spatial_w.astype(_BF16),
        a_bias=jnp.repeat(jnp.swapaxes(a_spatial_b, 1, 2), A_HEAD_DIM, axis=2),
        b_conv_w=b_conv_w, ab_w_out=ab_w_out.astype(_BF16),
        c_w_pw1=c_w_pw1.astype(_BF16), c_b_pw1=row(c_b_pw1),
        c_dw_w=dw_pairs.reshape(-1, CONV_W * V7X_BF16_SUBLANES, V7X_LANES),
        c_dw_b=row(c_dw_b), c_ln_g=row(c_ln_g), c_ln_b=row(c_ln_b),
        c_w_pw2=c_w_pw2.astype(_BF16), c_b_pw2=row(c_b_pw2),
    )


def _trunk(x, w):
    batch, seq_len, _ = x.shape
    assert seq_len % TM_FFN == 0 and seq_len % TM_AB == 0 and seq_len % TM_C == 0 and TM_AB % CHUNK == 0
    x = x.reshape(batch * seq_len, D_MODEL)
    depth = w["ffn_norm"].shape[0]
    for l in range(depth):
        x = _ffn(x, w, l, 0, final_norm=False)
        x = _mixer_ab(x, seq_len, w, l) if l % 2 == 0 else _conformer(x, seq_len, w, l)
        x = _ffn(x, w, l, 1, final_norm=(l == depth - 1))
    return x.reshape(batch, seq_len, D_MODEL)


def kernel(x_prompt, x_sample, ffn_norm, ffn_w_gate, ffn_w_up, ffn_w_down, mix_norm, ab_w_in, a_spatial_w, a_spatial_b, b_conv_w, ab_w_out, c_w_pw1, c_b_pw1, c_dw_w, c_dw_b, c_ln_g, c_ln_b, c_w_pw2, c_b_pw2, final_norm):
    w = _prep_weights(ffn_norm, ffn_w_gate, ffn_w_up, ffn_w_down, mix_norm, ab_w_in, a_spatial_w, a_spatial_b,
                      b_conv_w, ab_w_out, c_w_pw1, c_b_pw1, c_dw_w, c_dw_b, c_ln_g, c_ln_b, c_w_pw2, c_b_pw2,
                      final_norm)
    return (_trunk(x_prompt, w), _trunk(x_sample, w))
```
